```python
import math
import jax, jax.numpy as jnp
from jax import lax
import numpy as np

D_MODEL = 1024
BATCH = 8
SEQ = 4096
DEPTH = 1

D_MIX = D_MODEL
D_CONV = D_MIX // 2
D_ATTN = D_MIX - D_CONV
N_HEADS = 8
HEAD_DIM = D_ATTN // N_HEADS
D_IN_PROJ = 2 * D_CONV + 3 * D_ATTN
CONV_WIDTH = 31
ROPE_THETA = 10000.0
MOBA_BLOCK = 256
MOBA_TOPK = 3
Q_CHUNK = 32
N_EXPERTS = 256
TOP_K = 8
N_GROUP = 8
TOPK_GROUP = 4
D_EXPERT = 256
D_SHARED = 256
ROUTED_SCALE = 2.5
ROW_BLOCK = 128
BLOCKS_PER_STEP = 8
RMS_EPS = 1e-6
LN_EPS = 1e-5
NEG_INF = -1e30

kernel_name = 'hymba_conformer_moba_dsv3moe_adaln'


def rms_norm(x, g):
    xf = x.astype(jnp.float32)
    y = xf * lax.rsqrt(jnp.mean(xf * xf, axis=-1, keepdims=True) + RMS_EPS)
    return (y * g.astype(jnp.float32)).astype(x.dtype)


def layer_norm(x, g, b):
    xf = x.astype(jnp.float32)
    mu = jnp.mean(xf, axis=-1, keepdims=True)
    var = jnp.mean(jnp.square(xf - mu), axis=-1, keepdims=True)
    y = (xf - mu) * lax.rsqrt(var + LN_EPS)
    return (y * g.astype(jnp.float32) + b.astype(jnp.float32)).astype(x.dtype)


def modulate(x, g, shift, scale):
    return rms_norm(x, g) * (1 + scale[:, None, :]) + shift[:, None, :]


def rope(x):
    seq = x.shape[1]
    inv_freq = ROPE_THETA ** (-jnp.arange(0, HEAD_DIM, 2, dtype=jnp.float32) / HEAD_DIM)
    ang = jnp.arange(seq, dtype=jnp.float32)[:, None] * inv_freq[None, :]
    cos = jnp.cos(ang)[None, :, None, :]
    sin = jnp.sin(ang)[None, :, None, :]
    xf = x.astype(jnp.float32)
    x1, x2 = xf[..., :HEAD_DIM // 2], xf[..., HEAD_DIM // 2:]
    return jnp.concatenate([x1 * cos - x2 * sin, x2 * cos + x1 * sin], axis=-1).astype(x.dtype)


def conformer_conv_group(a, gate, conv_w, conv_b, ln_g, ln_b):
    u = a * jax.nn.sigmoid(gate)
    u = lax.conv_general_dilated(
        u, conv_w[:, None, :].astype(u.dtype), window_strides=(1,),
        padding=[(CONV_WIDTH - 1, 0)],
        dimension_numbers=('NWC', 'WIO', 'NWC'),
        feature_group_count=u.shape[-1]) + conv_b
    return jax.nn.silu(layer_norm(u, ln_g, ln_b))


def moba_attention(q, k, v):
    B, S, H, Dh = q.shape
    nb = -(-S // MOBA_BLOCK)
    pad = nb * MOBA_BLOCK - S
    qh = q.transpose(0, 2, 1, 3)
    kh = jnp.pad(k.transpose(0, 2, 1, 3), ((0, 0), (0, 0), (0, pad), (0, 0)))
    vh = jnp.pad(v.transpose(0, 2, 1, 3), ((0, 0), (0, 0), (0, pad), (0, 0)))
    k_blocks = kh.reshape(B, H, nb, MOBA_BLOCK, Dh)
    v_blocks = vh.reshape(B, H, nb, MOBA_BLOCK, Dh)
    k_mean = jnp.mean(k_blocks.astype(jnp.float32), axis=3)
    n_sel = min(MOBA_TOPK, nb)
    scale = 1.0 / math.sqrt(Dh)
    bi = jnp.arange(B)[:, None, None, None]
    hi = jnp.arange(H)[None, :, None, None]
    blk_ids = jnp.arange(nb)
    key_off = jnp.arange(MOBA_BLOCK)
    n_chunks = S // Q_CHUNK
    q_chunks = qh.reshape(B, H, n_chunks, Q_CHUNK, Dh).transpose(2, 0, 1, 3, 4)
    starts = jnp.arange(n_chunks, dtype=jnp.int32) * Q_CHUNK

    def attend_chunk(args):
        qc, start = args
        own = start // MOBA_BLOCK
        qpos = start + jnp.arange(Q_CHUNK)
        gate = jnp.einsum('bhqd,bhnd->bhqn', qc.astype(jnp.float32), k_mean)
        gate = jnp.where(blk_ids < own, gate, -jnp.inf)
        _, sel = lax.top_k(gate, n_sel)
        valid = sel < own
        k_sel = k_blocks[bi, hi, sel]
        v_sel = v_blocks[bi, hi, sel]
        s_sel = jnp.einsum('bhqd,bhqrkd->bhqrk', qc, k_sel,
                           preferred_element_type=jnp.float32) * scale
        s_sel = jnp.where(valid[..., None], s_sel, NEG_INF)
        k_own = lax.dynamic_index_in_dim(k_blocks, own, axis=2, keepdims=False)
        v_own = lax.dynamic_index_in_dim(v_blocks, own, axis=2, keepdims=False)
        s_own = jnp.einsum('bhqd,bhkd->bhqk', qc, k_own,
                           preferred_element_type=jnp.float32) * scale
        causal = (own * MOBA_BLOCK + key_off)[None, :] <= qpos[:, None]
        s_own = jnp.where(causal, s_own, NEG_INF)
        logits = jnp.concatenate(
            [s_sel.reshape(B, H, Q_CHUNK, n_sel * MOBA_BLOCK), s_own], axis=-1)
        p = jax.nn.softmax(logits, axis=-1).astype(v_blocks.dtype)
        p_sel = p[..., :n_sel * MOBA_BLOCK].reshape(B, H, Q_CHUNK, n_sel, MOBA_BLOCK)
        p_own = p[..., n_sel * MOBA_BLOCK:]
        return (jnp.einsum('bhqrk,bhqrkd->bhqd', p_sel, v_sel)
                + jnp.einsum('bhqk,bhkd->bhqd', p_own, v_own))

    out = lax.map(attend_chunk, (q_chunks, starts))
    return out.transpose(1, 0, 3, 2, 4).reshape(B, S, H, Dh)


def hybrid_mixer(h, w_in, conv_w, conv_b, conv_ln_g, conv_ln_b, q_norm_g, k_norm_g, grp_norm_g, w_out):
    B, S, _ = h.shape
    proj = h @ w_in
    a, gate, q, k, v = jnp.split(
        proj, [D_CONV, 2 * D_CONV, 2 * D_CONV + D_ATTN, 2 * D_CONV + 2 * D_ATTN], axis=-1)
    conv_out = conformer_conv_group(a, gate, conv_w, conv_b, conv_ln_g, conv_ln_b)
    q = rope(rms_norm(q.reshape(B, S, N_HEADS, HEAD_DIM), q_norm_g))
    k = rope(rms_norm(k.reshape(B, S, N_HEADS, HEAD_DIM), k_norm_g))
    v = v.reshape(B, S, N_HEADS, HEAD_DIM)
    attn_out = moba_attention(q, k, v).reshape(B, S, D_ATTN)
    merged = jnp.concatenate([rms_norm(conv_out, grp_norm_g[:D_CONV]),
                              rms_norm(attn_out, grp_norm_g[D_CONV:])], axis=-1)
    return merged @ w_out


def moe_ffn(h, w_router, router_bias, w_gate_e, w_up_e, w_down_e, w_gate_s, w_up_s, w_down_s):
    T, D = h.shape
    logits = jnp.matmul(h, w_router, preferred_element_type=jnp.float32)
    scores = jax.nn.sigmoid(logits)
    biased = scores + router_bias.astype(jnp.float32)
    grp = biased.reshape(T, N_GROUP, N_EXPERTS // N_GROUP)
    grp_score = jnp.sum(lax.top_k(grp, 2)[0], axis=-1)
    _, gidx = lax.top_k(grp_score, TOPK_GROUP)
    gmask = jnp.any(jax.nn.one_hot(gidx, N_GROUP, dtype=jnp.bool_), axis=1)
    emask = jnp.repeat(gmask, N_EXPERTS // N_GROUP, axis=1)
    _, eidx = lax.top_k(jnp.where(emask, biased, -jnp.inf), TOP_K)
    w = jnp.take_along_axis(scores, eidx, axis=1)
    w = w / jnp.sum(w, axis=-1, keepdims=True) * ROUTED_SCALE
    A = T * TOP_K
    flat_e = eidx.reshape(-1)
    flat_tok = jnp.repeat(jnp.arange(T, dtype=jnp.int32), TOP_K)
    flat_w = w.reshape(-1)
    order = jnp.argsort(flat_e)
    se, stok, sw = flat_e[order], flat_tok[order], flat_w[order]
    counts = jnp.bincount(flat_e, length=N_EXPERTS)
    offs = jnp.cumsum(counts) - counts
    pcounts = (counts + ROW_BLOCK - 1) // ROW_BLOCK * ROW_BLOCK
    pcum = jnp.cumsum(pcounts)
    poffs = pcum - pcounts
    dest = poffs[se] + (jnp.arange(A) - offs[se])
    n_blk = -(-A // ROW_BLOCK) + N_EXPERTS
    n_blk = -(-n_blk // BLOCKS_PER_STEP) * BLOCKS_PER_STEP
    P = n_blk * ROW_BLOCK
    tok_rows = jnp.zeros((P,), jnp.int32).at[dest].set(stok)
    w_rows = jnp.zeros((P,), jnp.float32).at[dest].set(sw)
    blk_e = jnp.minimum(jnp.searchsorted(pcum, jnp.arange(n_blk) * ROW_BLOCK, side='right'),
                        N_EXPERTS - 1)
    n_steps = n_blk // BLOCKS_PER_STEP
    tok_steps = tok_rows.reshape(n_steps, BLOCKS_PER_STEP, ROW_BLOCK)
    w_steps = w_rows.reshape(n_steps, BLOCKS_PER_STEP, ROW_BLOCK)
    e_steps = blk_e.reshape(n_steps, BLOCKS_PER_STEP)

    def expert_step(acc, args):
        tok, wr, eg = args
        xg = h[tok]
        g = jnp.einsum('gmd,gdf->gmf', xg, w_gate_e[eg])
        u = jnp.einsum('gmd,gdf->gmf', xg, w_up_e[eg])
        y = jnp.einsum('gmf,gfd->gmd', jax.nn.silu(g) * u, w_down_e[eg])
        y = y.astype(jnp.float32) * wr[..., None]
        return acc.at[tok.reshape(-1)].add(y.reshape(-1, D)), None

    routed, _ = lax.scan(expert_step, jnp.zeros((T, D), jnp.float32), (tok_steps, w_steps, e_steps))
    shared = (jax.nn.silu(h @ w_gate_s) * (h @ w_up_s)) @ w_down_s
    return shared + routed.astype(h.dtype)


def setup_inputs(seed: int = 0) -> dict:
    key = jax.random.key(seed)
    ks = jax.random.split(key, 23)
    L, D, E, F = DEPTH, D_MODEL, N_EXPERTS, D_EXPERT
    n = lambda k, shape: jax.random.normal(k, shape, jnp.float32)
    return {
        'x': n(ks[0], (BATCH, SEQ, D)),
        'c': n(ks[1], (BATCH, D)),
        'w_ada': n(ks[2], (L, D, 6 * D)) * (0.5 * D ** -0.5),
        'b_ada': n(ks[3], (L, 6 * D)) * 0.02,
        'norm1_g': 1.0 + 0.05 * n(ks[4], (L, D)),
        'w_in': n(ks[5], (L, D, D_IN_PROJ)) * D ** -0.5,
        'conv_w': n(ks[6], (L, CONV_WIDTH, D_CONV)) * CONV_WIDTH ** -0.5,
        'conv_b': n(ks[7], (L, D_CONV)) * 0.02,
        'conv_ln_g': 1.0 + 0.05 * n(ks[8], (L, D_CONV)),
        'conv_ln_b': n(ks[9], (L, D_CONV)) * 0.02,
        'q_norm_g': 1.0 + 0.05 * n(ks[10], (L, HEAD_DIM)),
        'k_norm_g': 1.0 + 0.05 * n(ks[11], (L, HEAD_DIM)),
        'grp_norm_g': 1.0 + 0.05 * n(ks[12], (L, D_MIX)),
        'w_out': n(ks[13], (L, D_MIX, D)) * D_MIX ** -0.5,
        'norm2_g': 1.0 + 0.05 * n(ks[14], (L, D)),
        'w_router': n(ks[15], (L, D, E)) * D ** -0.5,
        'router_bias': n(ks[16], (L, E)) * 0.01,
        'w_gate_e': n(ks[17], (L, E, D, F)) * D ** -0.5,
        'w_up_e': n(ks[18], (L, E, D, F)) * D ** -0.5,
        'w_down_e': n(ks[19], (L, E, F, D)) * F ** -0.5,
        'w_gate_s': n(ks[20], (L, D, D_SHARED)) * D ** -0.5,
        'w_up_s': n(ks[21], (L, D, D_SHARED)) * D ** -0.5,
        'w_down_s': n(ks[22], (L, D_SHARED, D)) * D_SHARED ** -0.5,
    }


def reference(x, c, w_ada, b_ada, norm1_g, w_in, conv_w, conv_b, conv_ln_g, conv_ln_b,
              q_norm_g, k_norm_g, grp_norm_g, w_out, norm2_g, w_router, router_bias,
              w_gate_e, w_up_e, w_down_e, w_gate_s, w_up_s, w_down_s):
    B, S, D = x.shape
    for l in range(DEPTH):
        mod = jax.nn.silu(c) @ w_ada[l] + b_ada[l]
        sh1, sc1, g1, sh2, sc2, g2 = jnp.split(mod, 6, axis=-1)
        h = modulate(x, norm1_g[l], sh1, sc1)
        mix = hybrid_mixer(h, w_in[l], conv_w[l], conv_b[l], conv_ln_g[l], conv_ln_b[l],
                           q_norm_g[l], k_norm_g[l], grp_norm_g[l], w_out[l])
        x = x + g1[:, None, :] * mix
        h = modulate(x, norm2_g[l], sh2, sc2)
        ffn = moe_ffn(h.reshape(B * S, D), w_router[l], router_bias[l], w_gate_e[l], w_up_e[l],
                      w_down_e[l], w_gate_s[l], w_up_s[l], w_down_s[l])
        x = x + g2[:, None, :] * ffn.reshape(B, S, D)
    return x
```

```python
import functools
import math

import jax
import jax.numpy as jnp
from jax import lax
from jax.experimental import pallas as pl
from jax.experimental.pallas import tpu as pltpu

F32 = jnp.float32
BF16 = jnp.bfloat16
I32 = jnp.int32

N_HEADS = 8
HEAD_DIM = 64
D_CONV = 512
D_ATTN = 512
CONV_WIDTH = 31
CONV_HALO = 32
ROPE_THETA = 10000.0
MOBA_BLOCK = 256
MOBA_TOPK = 3
N_EXPERTS = 256
TOP_K = 8
N_GROUP = 8
TOPK_GROUP = 4
ROUTED_SCALE = 2.5
RMS_EPS = 1e-6
LN_EPS = 1e-5
NEG_INF = -1e30

EXPERT_ROWS = 256
VMEM_LIMIT = 56 * 1024 * 1024


def _cparams(sem):
    return pltpu.CompilerParams(dimension_semantics=sem, vmem_limit_bytes=VMEM_LIMIT)


def _sigmoid(x):
    return 1.0 / (1.0 + jnp.exp(-x))


def _split_bf16(x):
    hi = x.astype(BF16)
    lo = (x - hi.astype(F32)).astype(BF16)
    return hi, lo


def _ada_kernel(c_ref, w_ref, b_ref, o_ref):
    c = c_ref[...]
    cs = c * _sigmoid(c)
    o_ref[...] = jnp.dot(cs, w_ref[...], preferred_element_type=F32,
                         precision=lax.Precision.HIGHEST) + b_ref[...]


def _ada(c, w_ada, b_ada):
    B, D = c.shape
    n = w_ada.shape[1]
    return pl.pallas_call(
        _ada_kernel,
        grid=(n // D,),
        in_specs=[pl.BlockSpec((B, D), lambda j: (0, 0)),
                  pl.BlockSpec((D, D), lambda j: (0, j)),
                  pl.BlockSpec((1, D), lambda j: (0, j))],
        out_specs=pl.BlockSpec((B, D), lambda j: (0, j)),
        out_shape=jax.ShapeDtypeStruct((B, n), F32),
        compiler_params=_cparams(("arbitrary",)),
        name="ada",
    )(c, w_ada, b_ada.reshape(1, n))


def _modulated_norm(x, g, shift, scale):
    ms = jnp.mean(x * x, axis=-1, keepdims=True)
    return (x * lax.rsqrt(ms + RMS_EPS) * g) * (1.0 + scale) + shift


def _inproj_kernel(x_ref, sh_ref, sc_ref, g_ref, w_ref, gq_ref, gk_ref, cos_ref, sin_ref, grp_ref,
                   u_ref, q_ref, k_ref, v_ref, km_ref):
    ts = x_ref.shape[1]
    h = _modulated_norm(x_ref[0], g_ref[...], sh_ref[0], sc_ref[0])
    hb = h.astype(BF16)

    def proj(lo):
        return jnp.dot(hb, w_ref[:, lo:lo + 512], preferred_element_type=F32)

    u_ref[0] = proj(0) * _sigmoid(proj(D_CONV))

    cos = cos_ref[...]
    sin = sin_ref[...]
    lane = lax.broadcasted_iota(I32, (1, D_ATTN), 1)
    first_half = (lane % HEAD_DIM) < (HEAD_DIM // 2)
    grp = grp_ref[...]

    def norm_rope(t, g):
        hi, lo = _split_bf16(t * t)
        ss = (jnp.dot(hi, grp, preferred_element_type=F32)
              + jnp.dot(lo, grp, preferred_element_type=F32))
        tn = t * lax.rsqrt(ss * (1.0 / HEAD_DIM) + RMS_EPS) * g
        partner = jnp.where(first_half,
                            pltpu.roll(tn, D_ATTN - HEAD_DIM // 2, 1),
                            pltpu.roll(tn, HEAD_DIM // 2, 1))
        return tn * cos + partner * sin

    q = norm_rope(proj(2 * D_CONV), gq_ref[...])
    k = norm_rope(proj(2 * D_CONV + D_ATTN), gk_ref[...])
    q_ref[0] = q.astype(BF16)
    k_ref[0] = k.astype(BF16)
    v_ref[0] = proj(2 * D_CONV + 2 * D_ATTN).astype(BF16)
    for j in range(ts // MOBA_BLOCK):
        km_ref[0, j] = jnp.mean(k[j * MOBA_BLOCK:(j + 1) * MOBA_BLOCK], axis=0, keepdims=True)


def _in_proj(x, sh1, sc1, norm1_g, w_in_bf, gq, gk, cos_t, sin_t, grp_ones, ts):
    B, S, D = x.shape
    nS = S // ts
    nbt = ts // MOBA_BLOCK
    row = lambda i, b: (b, i, 0)
    const = lambda i, b: (0, 0)
    return pl.pallas_call(
        _inproj_kernel,
        grid=(nS, B),
        in_specs=[pl.BlockSpec((1, ts, D), row),
                  pl.BlockSpec((1, 1, D), lambda i, b: (b, 0, 0)),
                  pl.BlockSpec((1, 1, D), lambda i, b: (b, 0, 0)),
                  pl.BlockSpec((1, D), const),
                  pl.BlockSpec(w_in_bf.shape, const),
                  pl.BlockSpec((1, D_ATTN), const),
                  pl.BlockSpec((1, D_ATTN), const),
                  pl.BlockSpec((ts, D_ATTN), lambda i, b: (i, 0)),
                  pl.BlockSpec((ts, D_ATTN), lambda i, b: (i, 0)),
                  pl.BlockSpec((D_ATTN, D_ATTN), const)],
        out_specs=[pl.BlockSpec((1, ts, D_CONV), row),
                   pl.BlockSpec((1, ts, D_ATTN), row),
                   pl.BlockSpec((1, ts, D_ATTN), row),
                   pl.BlockSpec((1, ts, D_ATTN), row),
                   pl.BlockSpec((1, nbt, 1, D_ATTN), lambda i, b: (b, i, 0, 0))],
        out_shape=[jax.ShapeDtypeStruct((B, S, D_CONV), F32),
                   jax.ShapeDtypeStruct((B, S, D_ATTN), BF16),
                   jax.ShapeDtypeStruct((B, S, D_ATTN), BF16),
                   jax.ShapeDtypeStruct((B, S, D_ATTN), BF16),
                   jax.ShapeDtypeStruct((B, S // MOBA_BLOCK, 1, D_ATTN), F32)],
        compiler_params=_cparams(("arbitrary", "arbitrary")),
        name="in_proj",
    )(x, sh1, sc1, norm1_g, w_in_bf, gq, gk, cos_t, sin_t, grp_ones)


def _conv_kernel(u_ref, halo_ref, w_ref, b_ref, lng_ref, lnb_ref, gg_ref, o_ref, buf_ref):
    ts = u_ref.shape[1]
    i = pl.program_id(1)
    halo = halo_ref[0]
    buf_ref[0:CONV_HALO] = jnp.where(i == 0, jnp.zeros_like(halo), halo)
    buf_ref[CONV_HALO:CONV_HALO + ts] = u_ref[0]
    base = CONV_HALO - (CONV_WIDTH - 1)
    acc = jnp.broadcast_to(b_ref[...], (ts, D_CONV))
    for j in range(CONV_WIDTH):
        acc = acc + w_ref[j:j + 1, :] * buf_ref[base + j:base + j + ts, :]
    mu = jnp.mean(acc, axis=-1, keepdims=True)
    xc = acc - mu
    var = jnp.mean(xc * xc, axis=-1, keepdims=True)
    y = xc * lax.rsqrt(var + LN_EPS) * lng_ref[...] + lnb_ref[...]
    s = y * _sigmoid(y)
    ms = jnp.mean(s * s, axis=-1, keepdims=True)
    o_ref[0] = (s * lax.rsqrt(ms + RMS_EPS) * gg_ref[...]).astype(o_ref.dtype)


def _conv_group(u, conv_w, conv_b, ln_g, ln_b, gg, ts):
    B, S, C = u.shape
    nS = S // ts
    hpt = ts // CONV_HALO
    const = lambda b, i: (0, 0)
    return pl.pallas_call(
        _conv_kernel,
        grid=(B, nS),
        in_specs=[pl.BlockSpec((1, ts, C), lambda b, i: (b, i, 0)),
                  pl.BlockSpec((1, CONV_HALO, C), lambda b, i: (b, jnp.maximum(i * hpt - 1, 0), 0)),
                  pl.BlockSpec((CONV_WIDTH, C), const),
                  pl.BlockSpec((1, C), const),
                  pl.BlockSpec((1, C), const),
                  pl.BlockSpec((1, C), const),
                  pl.BlockSpec((1, C), const)],
        out_specs=pl.BlockSpec((1, ts, C), lambda b, i: (b, i, 0)),
        out_shape=jax.ShapeDtypeStruct((B, S, C), BF16),
        scratch_shapes=[pltpu.VMEM((ts + CONV_HALO, C), F32)],
        compiler_params=_cparams(("arbitrary", "arbitrary")),
        name="conv",
    )(u, u, conv_w, conv_b, ln_g, ln_b, gg)


def _attn_kernel(q_ref, k_ref, v_ref, km_ref, o_ref, bias_ref):
    nb = km_ref.shape[1]
    qi = pl.program_id(2)
    blk = MOBA_BLOCK
    q = q_ref[0] * (1.0 / math.sqrt(HEAD_DIM))
    lane = lax.broadcasted_iota(I32, (1, 2 * HEAD_DIM), 1)
    km = km_ref[0]
    col = lax.broadcasted_iota(I32, (blk, nb), 1)
    past = col < qi
    nt = (((1,), (1,)), ((), ()))
    rows = lax.broadcasted_iota(I32, (blk, blk), 0)
    cols = lax.broadcasted_iota(I32, (blk, blk), 1)
    causal = cols <= rows
    k_own = k_ref[0, pl.ds(pl.multiple_of(qi * blk, blk), blk), :]
    v_own = v_ref[0, pl.ds(pl.multiple_of(qi * blk, blk), blk), :]

    outs = []
    for h in range(2):
        head = (lane // HEAD_DIM) == h
        qh = jnp.where(head, q, jnp.zeros_like(q))
        kmh = jnp.where(head, km, 0.0)
        k1 = kmh.astype(BF16)
        r1 = kmh - k1.astype(F32)
        k2 = r1.astype(BF16)
        k3 = (r1 - k2.astype(F32)).astype(BF16)
        gate = (lax.dot_general(qh, k1, nt, preferred_element_type=F32)
                + lax.dot_general(qh, k2, nt, preferred_element_type=F32)
                + lax.dot_general(qh, k3, nt, preferred_element_type=F32))
        gate = jnp.where(past, gate, -jnp.inf)
        rank = jnp.zeros((blk, nb), I32)
        for j in range(nb):
            gj = gate[:, j:j + 1]
            beats = (gj > gate) | ((gj == gate) & (j < col))
            rank = rank + beats.astype(I32)
        sel = (rank < MOBA_TOPK) & past
        bias = jnp.where(sel, 0.0, NEG_INF)
        for j in range(nb):
            bias_ref[h, j] = jnp.broadcast_to(bias[:, j:j + 1], (blk, 2 * HEAD_DIM))

        s = lax.dot_general(qh, k_own, nt, preferred_element_type=F32)
        s = jnp.where(causal, s, NEG_INF)
        m = jnp.max(s, axis=-1, keepdims=True)
        p = jnp.exp(s - m)
        l = jnp.sum(p, axis=-1, keepdims=True)
        acc = jnp.dot(p.astype(BF16), v_own, preferred_element_type=F32)

        def body(j, carry, qh=qh, h=h):
            m, l, acc = carry
            start = pl.multiple_of(j * blk, blk)
            kj = k_ref[0, pl.ds(start, blk), :]
            vj = v_ref[0, pl.ds(start, blk), :]
            b = bias_ref[h, j]
            s = lax.dot_general(qh, kj, nt, preferred_element_type=F32) + jnp.concatenate([b, b], axis=1)
            m_new = jnp.maximum(m, jnp.max(s, axis=-1, keepdims=True))
            alpha = jnp.exp(m - m_new)
            p = jnp.exp(s - m_new)
            l = alpha * l + jnp.sum(p, axis=-1, keepdims=True)
            acc = alpha * acc + jnp.dot(p.astype(BF16), vj, preferred_element_type=F32)
            return m_new, l, acc

        m, l, acc = lax.fori_loop(0, qi, body, (m, l, acc))
        outs.append(acc / l)

    o_ref[0] = jnp.where((lane // HEAD_DIM) == 0, outs[0], outs[1]).astype(o_ref.dtype)


def _moba_attention(q, k, v, kmean):
    B, S, _ = q.shape
    nb = S // MOBA_BLOCK
    npair = N_HEADS // 2
    w = 2 * HEAD_DIM
    return pl.pallas_call(
        _attn_kernel,
        grid=(B, npair, nb),
        in_specs=[pl.BlockSpec((1, MOBA_BLOCK, w), lambda b, p, i: (b, i, p)),
                  pl.BlockSpec((1, S, w), lambda b, p, i: (b, 0, p)),
                  pl.BlockSpec((1, S, w), lambda b, p, i: (b, 0, p)),
                  pl.BlockSpec((1, nb, w), lambda b, p, i: (b, 0, p))],
        out_specs=pl.BlockSpec((1, MOBA_BLOCK, w), lambda b, p, i: (b, i, p)),
        out_shape=jax.ShapeDtypeStruct((B, S, D_ATTN), BF16),
        scratch_shapes=[pltpu.VMEM((2, nb, MOBA_BLOCK, w), F32)],
        compiler_params=_cparams(("arbitrary", "arbitrary", "arbitrary")),
        name="moba_attn",
    )(q, k, v, kmean)


def _route(logits_t, bias_col):
    n = logits_t.shape[1]
    per = N_EXPERTS // N_GROUP
    scores = _sigmoid(logits_t)
    biased = scores + bias_col
    g3 = biased.reshape(N_GROUP, per, n)
    sub = lax.broadcasted_iota(I32, (N_GROUP, per, n), 1)
    m1 = jnp.max(g3, axis=1, keepdims=True)
    first = jnp.min(jnp.where(g3 == m1, sub, per), axis=1, keepdims=True)
    m2 = jnp.max(jnp.where(sub == first, -jnp.inf, g3), axis=1)
    gs = m1[:, 0, :] + m2
    gi = lax.broadcasted_iota(I32, (N_GROUP, n), 0)
    grank = jnp.zeros((N_GROUP, n), I32)
    for j in range(N_GROUP):
        row = gs[j:j + 1, :]
        grank = grank + ((row > gs) | ((row == gs) & (j < gi))).astype(I32)
    gmask = grank < TOPK_GROUP
    masked = jnp.where(gmask[:, None, :], g3, -jnp.inf).reshape(N_EXPERTS, n)
    ei = lax.broadcasted_iota(I32, (N_EXPERTS, n), 0)
    ids, ws = [], []
    onehot = jnp.zeros((N_EXPERTS, n), F32)
    for _ in range(TOP_K):
        m = jnp.max(masked, axis=0, keepdims=True)
        idx = jnp.min(jnp.where(masked == m, ei, N_EXPERTS), axis=0, keepdims=True)
        hit = ei == idx
        ws.append(jnp.sum(jnp.where(hit, scores, 0.0), axis=0, keepdims=True))
        ids.append(idx)
        masked = jnp.where(hit, -jnp.inf, masked)
        onehot = onehot + hit.astype(F32)
    w = jnp.concatenate(ws, axis=0)
    w = w / jnp.sum(w, axis=0, keepdims=True) * ROUTED_SCALE
    return jnp.concatenate(ids, axis=0), w, onehot


def _outproj_kernel(conv_ref, attn_ref, x_ref, g1_ref, gga_ref, wo_ref, n2_ref, sh_ref, sc_ref,
                    wr_ref, rb_ref, tri_ref,
                    x1_ref, h2_ref, eid_ref, wt_ref, rank_ref, cnt_ref, run_ref):
    first_step = (pl.program_id(0) == 0) & (pl.program_id(1) == 0)

    @pl.when(first_step)
    def _():
        run_ref[...] = jnp.zeros_like(run_ref)

    at = attn_ref[0].astype(F32)
    ms = jnp.mean(at * at, axis=-1, keepdims=True)
    atn = (at * lax.rsqrt(ms + RMS_EPS) * gga_ref[...]).astype(BF16)
    mix = (jnp.dot(conv_ref[0], wo_ref[0:D_CONV, :], preferred_element_type=F32)
           + jnp.dot(atn, wo_ref[D_CONV:D_CONV + D_ATTN, :], preferred_element_type=F32))
    x1 = x_ref[0] + g1_ref[0] * mix
    x1_ref[0] = x1
    h2 = _modulated_norm(x1, n2_ref[...], sh_ref[0], sc_ref[0])
    h2_ref[0] = h2

    nt = (((1,), (1,)), ((), ()))
    hh, hl = _split_bf16(h2)
    wh, wl = _split_bf16(wr_ref[...])
    logits_t = (lax.dot_general(wh, hh, nt, preferred_element_type=F32)
                + lax.dot_general(wh, hl, nt, preferred_element_type=F32)
                + lax.dot_general(wl, hh, nt, preferred_element_type=F32))
    ids, w, onehot = _route(logits_t, rb_ref[...])
    eid_ref[...] = ids
    wt_ref[...] = w

    prefix = jnp.dot(onehot.astype(BF16), tri_ref[...], preferred_element_type=F32)
    slot = prefix + run_ref[...]
    ei = lax.broadcasted_iota(I32, onehot.shape, 0)
    ranks = [jnp.sum(jnp.where(ei == ids[r:r + 1, :], slot, 0.0), axis=0, keepdims=True)
             for r in range(TOP_K)]
    rank_ref[...] = jnp.concatenate(ranks, axis=0).astype(I32)
    run_ref[...] = run_ref[...] + jnp.sum(onehot, axis=1, keepdims=True)
    cnt_ref[...] = jnp.broadcast_to(run_ref[...], cnt_ref.shape)


def _out_proj_route(conv_o, attn_o, x, g1, gg_attn, w_out_bf, norm2_g, sh2, sc2, w_router_t,
                    router_bias, tri, ts):
    B, S, D = x.shape
    nS = S // ts
    T = B * S
    row = lambda b, i: (b, i, 0)
    perb = lambda b, i: (b, 0, 0)
    const = lambda b, i: (0, 0)
    tok = lambda b, i: (0, b * nS + i)
    return pl.pallas_call(
        _outproj_kernel,
        grid=(B, nS),
        in_specs=[pl.BlockSpec((1, ts, D_CONV), row),
                  pl.BlockSpec((1, ts, D_ATTN), row),
                  pl.BlockSpec((1, ts, D), row),
                  pl.BlockSpec((1, 1, D), perb),
                  pl.BlockSpec((1, D_ATTN), const),
                  pl.BlockSpec((D, D), const),
                  pl.BlockSpec((1, D), const),
                  pl.BlockSpec((1, 1, D), perb),
                  pl.BlockSpec((1, 1, D), perb),
                  pl.BlockSpec((N_EXPERTS, D), const),
                  pl.BlockSpec((N_EXPERTS, 1), const),
                  pl.BlockSpec((ts, ts), const)],
        out_specs=[pl.BlockSpec((1, ts, D), row),
                   pl.BlockSpec((1, ts, D), row),
                   pl.BlockSpec((TOP_K, ts), tok),
                   pl.BlockSpec((TOP_K, ts), tok),
                   pl.BlockSpec((TOP_K, ts), tok),
                   pl.BlockSpec((N_EXPERTS, 128), const)],
        out_shape=[jax.ShapeDtypeStruct((B, S, D), F32),
                   jax.ShapeDtypeStruct((B, S, D), F32),
                   jax.ShapeDtypeStruct((TOP_K, T), I32),
                   jax.ShapeDtypeStruct((TOP_K, T), F32),
                   jax.ShapeDtypeStruct((TOP_K, T), I32),
                   jax.ShapeDtypeStruct((N_EXPERTS, 128), F32)],
        scratch_shapes=[pltpu.VMEM((N_EXPERTS, 1), F32)],
        compiler_params=_cparams(("arbitrary", "arbitrary")),
        name="out_proj_route",
    )(conv_o, attn_o, x, g1, gg_attn, w_out_bf, norm2_g, sh2, sc2, w_router_t, router_bias, tri)


def _dispatch_kernel(dest_ref, h_ref, xs_in_ref, xs_ref, sem):
    del xs_in_ref
    tq = h_ref.shape[0]

    def row_copy(t, k):
        return pltpu.make_async_copy(h_ref.at[pl.ds(t, 1)], xs_ref.at[pl.ds(dest_ref[k, t], 1)], sem)

    def issue(t, _):
        for k in range(TOP_K):
            row_copy(t, k).start()
        return 0

    lax.fori_loop(0, tq, issue, 0)

    def drain(t, _):
        for k in range(TOP_K):
            row_copy(t, k).wait()
        return 0

    lax.fori_loop(0, tq, drain, 0)


def _dispatch(dest, h2, n_rows, tq):
    T, D = h2.shape
    xs0 = jnp.zeros((n_rows, D), F32)
    return pl.pallas_call(
        _dispatch_kernel,
        grid=(T // tq,),
        in_specs=[pl.BlockSpec((TOP_K, tq), lambda i: (0, i), memory_space=pltpu.SMEM),
                  pl.BlockSpec((tq, D), lambda i: (i, 0)),
                  pl.BlockSpec(memory_space=pl.ANY)],
        out_specs=pl.BlockSpec(memory_space=pl.ANY),
        out_shape=jax.ShapeDtypeStruct((n_rows, D), F32),
        scratch_shapes=[pltpu.SemaphoreType.DMA(())],
        input_output_aliases={2: 0},
        compiler_params=_cparams(("arbitrary",)),
        name="moe_dispatch",
    )(dest, h2, xs0)


def _experts_kernel(be_ref, nused_ref, xs_ref, wg_ref, wu_ref, wd_ref, ys_ref, wg_s, wu_s, wd_s):
    i = pl.program_id(0)
    changed = (i == 0) | (be_ref[i] != be_ref[jnp.maximum(i - 1, 0)])

    @pl.when(changed)
    def _():
        wg_s[...] = wg_ref[0].astype(BF16)
        wu_s[...] = wu_ref[0].astype(BF16)
        wd_s[...] = wd_ref[0].astype(BF16)

    @pl.when(i < nused_ref[0])
    def _():
        x = xs_ref[...].astype(BF16)
        g = jnp.dot(x, wg_s[...], preferred_element_type=F32)
        u = jnp.dot(x, wu_s[...], preferred_element_type=F32)
        a = (g * _sigmoid(g) * u).astype(BF16)
        ys_ref[...] = jnp.dot(a, wd_s[...], preferred_element_type=F32)

    @pl.when(i >= nused_ref[0])
    def _():
        ys_ref[...] = jnp.zeros_like(ys_ref)


def _experts(blk_e, nused, xs, w_gate_e, w_up_e, w_down_e):
    P, D = xs.shape
    F = w_gate_e.shape[2]
    nblk = P // EXPERT_ROWS
    grid_spec = pltpu.PrefetchScalarGridSpec(
        num_scalar_prefetch=2,
        grid=(nblk,),
        in_specs=[pl.BlockSpec((EXPERT_ROWS, D), lambda i, be, nu: (i, 0)),
                  pl.BlockSpec((1, D, F), lambda i, be, nu: (be[i], 0, 0)),
                  pl.BlockSpec((1, D, F), lambda i, be, nu: (be[i], 0, 0)),
                  pl.BlockSpec((1, F, D), lambda i, be, nu: (be[i], 0, 0))],
        out_specs=pl.BlockSpec((EXPERT_ROWS, D), lambda i, be, nu: (i, 0)),
        scratch_shapes=[pltpu.VMEM((D, F), BF16), pltpu.VMEM((D, F), BF16), pltpu.VMEM((F, D), BF16)],
    )
    return pl.pallas_call(
        _experts_kernel,
        grid_spec=grid_spec,
        out_shape=jax.ShapeDtypeStruct((P, D), F32),
        compiler_params=_cparams(("arbitrary",)),
        name="moe_experts",
    )(blk_e, nused, xs, w_gate_e, w_up_e, w_down_e)


def _combine_kernel(dest_ref, w_ref, x1_ref, h2_ref, g2_ref, wgs_ref, wus_ref, wds_ref, ys_ref,
                    o_ref, buf_ref, sem):
    tc = x1_ref.shape[0]

    def row_copy(t, k):
        return pltpu.make_async_copy(ys_ref.at[pl.ds(dest_ref[k, t], 1)],
                                     buf_ref.at[k, pl.ds(t, 1)], sem)

    def issue(t, _):
        for k in range(TOP_K):
            row_copy(t, k).start()
        return 0

    lax.fori_loop(0, tc, issue, 0)

    hb = h2_ref[...].astype(BF16)
    g = jnp.dot(hb, wgs_ref[...], preferred_element_type=F32)
    u = jnp.dot(hb, wus_ref[...], preferred_element_type=F32)
    shared = jnp.dot((g * _sigmoid(g) * u).astype(BF16), wds_ref[...], preferred_element_type=F32)

    def drain(t, _):
        for k in range(TOP_K):
            row_copy(t, k).wait()
        return 0

    lax.fori_loop(0, tc, drain, 0)

    w = w_ref[...]
    routed = w[:, 0:1] * buf_ref[0]
    for k in range(1, TOP_K):
        routed = routed + w[:, k:k + 1] * buf_ref[k]
    o_ref[...] = x1_ref[...] + g2_ref[0] * (shared + routed)


def _combine(dest, w_tok, x1, h2, g2, wgs, wus, wds, ys, tc, S):
    T, D = x1.shape
    F = wgs.shape[1]
    const = lambda i: (0, 0)
    return pl.pallas_call(
        _combine_kernel,
        grid=(T // tc,),
        in_specs=[pl.BlockSpec((TOP_K, tc), lambda i: (0, i), memory_space=pltpu.SMEM),
                  pl.BlockSpec((tc, TOP_K), lambda i: (i, 0)),
                  pl.BlockSpec((tc, D), lambda i: (i, 0)),
                  pl.BlockSpec((tc, D), lambda i: (i, 0)),
                  pl.BlockSpec((1, 1, D), lambda i: ((i * tc) // S, 0, 0)),
                  pl.BlockSpec((D, F), const),
                  pl.BlockSpec((D, F), const),
                  pl.BlockSpec((F, D), const),
                  pl.BlockSpec(memory_space=pl.ANY)],
        out_specs=pl.BlockSpec((tc, D), lambda i: (i, 0)),
        out_shape=jax.ShapeDtypeStruct((T, D), F32),
        scratch_shapes=[pltpu.VMEM((TOP_K, tc, D), F32), pltpu.SemaphoreType.DMA(())],
        compiler_params=_cparams(("arbitrary",)),
        name="moe_combine",
    )(dest, w_tok, x1, h2, g2, wgs, wus, wds, ys)


def _rope_tables(S):
    inv_freq = ROPE_THETA ** (-jnp.arange(0, HEAD_DIM, 2, dtype=F32) / HEAD_DIM)
    ang = jnp.arange(S, dtype=F32)[:, None] * inv_freq[None, :]
    cos, sin = jnp.cos(ang), jnp.sin(ang)
    cos_t = jnp.tile(jnp.concatenate([cos, cos], axis=1), (1, N_HEADS))
    sin_t = jnp.tile(jnp.concatenate([-sin, sin], axis=1), (1, N_HEADS))
    return cos_t, sin_t


def _layer(x, mod, norm1_g, w_in, conv_w, conv_b, conv_ln_g, conv_ln_b, q_norm_g, k_norm_g,
           grp_norm_g, w_out, norm2_g, w_router, router_bias, w_gate_e, w_up_e, w_down_e,
           w_gate_s, w_up_s, w_down_s):
    B, S, D = x.shape
    T = B * S
    ts = min(512, S)
    sh1, sc1, g1, sh2, sc2, g2 = [m.reshape(B, 1, D) for m in jnp.split(mod, 6, axis=-1)]
    cos_t, sin_t = _rope_tables(S)
    head_of = jnp.arange(D_ATTN) // HEAD_DIM
    grp_ones = (head_of[:, None] == head_of[None, :]).astype(BF16)
    row = lambda a: a.reshape(1, -1)

    u, q, k, v, kmean = _in_proj(x, sh1, sc1, row(norm1_g), w_in.astype(BF16),
                                 row(jnp.tile(q_norm_g, N_HEADS)), row(jnp.tile(k_norm_g, N_HEADS)),
                                 cos_t, sin_t, grp_ones, ts)
    conv_o = _conv_group(u, conv_w, row(conv_b), row(conv_ln_g), row(conv_ln_b),
                         row(grp_norm_g[:D_CONV]), ts)
    attn_o = _moba_attention(q, k, v, kmean.reshape(B, S // MOBA_BLOCK, D_ATTN))

    tri = (jnp.arange(ts)[:, None] < jnp.arange(ts)[None, :]).astype(BF16)
    x1, h2, eid_t, w_t, rank_t, cnt = _out_proj_route(
        conv_o, attn_o, x, g1, row(grp_norm_g[D_CONV:]), w_out.astype(BF16), row(norm2_g), sh2, sc2,
        w_router.T, router_bias.reshape(N_EXPERTS, 1), tri, ts)

    counts = cnt[:, 0].astype(I32)
    pcounts = (counts + EXPERT_ROWS - 1) // EXPERT_ROWS * EXPERT_ROWS
    pcum = jnp.cumsum(pcounts)
    poffs = pcum - pcounts
    dest = poffs[eid_t] + rank_t
    n_rows = T * TOP_K + N_EXPERTS * EXPERT_ROWS
    nblk = n_rows // EXPERT_ROWS
    blk_e = jnp.minimum(jnp.searchsorted(pcum, jnp.arange(nblk, dtype=I32) * EXPERT_ROWS, side='right'),
                        N_EXPERTS - 1).astype(I32)
    nused = (pcum[-1:] // EXPERT_ROWS).astype(I32)

    h2f = h2.reshape(T, D)
    xs = _dispatch(dest, h2f, n_rows, min(512, T))
    ys = _experts(blk_e, nused, xs, w_gate_e, w_up_e, w_down_e)
    out = _combine(dest, w_t.T, x1.reshape(T, D), h2f, g2,
                   w_gate_s.astype(BF16), w_up_s.astype(BF16), w_down_s.astype(BF16), ys,
                   min(128, T), S)
    return out.reshape(B, S, D)


def kernel(x, c, w_ada, b_ada, norm1_g, w_in, conv_w, conv_b, conv_ln_g, conv_ln_b, q_norm_g, k_norm_g,
           grp_norm_g, w_out, norm2_g, w_router, router_bias, w_gate_e, w_up_e, w_down_e,
           w_gate_s, w_up_s, w_down_s):
    for l in range(w_ada.shape[0]):
        mod = _ada(c, w_ada[l], b_ada[l])
        x = _layer(x, mod, norm1_g[l], w_in[l], conv_w[l], conv_b[l], conv_ln_g[l], conv_ln_b[l],
                   q_norm_g[l], k_norm_g[l], grp_norm_g[l], w_out[l], norm2_g[l], w_router[l],
                   router_bias[l], w_gate_e[l], w_up_e[l], w_down_e[l], w_gate_s[l], w_up_s[l],
                   w_down_s[l])
    return x
```

```python
import functools
import math

import jax
import jax.numpy as jnp
from jax import lax
from jax.experimental import pallas as pl
from jax.experimental.pallas import tpu as pltpu

F32 = jnp.float32
BF16 = jnp.bfloat16
I32 = jnp.int32

N_HEADS = 8
HEAD_DIM = 64
D_CONV = 512
D_ATTN = 512
CONV_WIDTH = 31
CONV_HALO = 32
ROPE_THETA = 10000.0
MOBA_BLOCK = 256
MOBA_TOPK = 3
N_EXPERTS = 256
TOP_K = 8
N_GROUP = 8
TOPK_GROUP = 4
ROUTED_SCALE = 2.5
RMS_EPS = 1e-6
LN_EPS = 1e-5
NEG_INF = -1e30

EXPERT_ROWS = 256
VMEM_LIMIT = 56 * 1024 * 1024


def _cparams(sem):
    return pltpu.CompilerParams(dimension_semantics=sem, vmem_limit_bytes=VMEM_LIMIT)


def _sigmoid(x):
    return 1.0 / (1.0 + jnp.exp(-x))


def _split_bf16(x):
    hi = x.astype(BF16)
    lo = (x - hi.astype(F32)).astype(BF16)
    return hi, lo


def _ada_kernel(c_ref, w_ref, b_ref, o_ref):
    c = c_ref[...]
    cs = c * _sigmoid(c)
    o_ref[...] = jnp.dot(cs, w_ref[...], preferred_element_type=F32,
                         precision=lax.Precision.HIGHEST) + b_ref[...]


def _ada(c, w_ada, b_ada):
    B, D = c.shape
    n = w_ada.shape[1]
    return pl.pallas_call(
        _ada_kernel,
        grid=(n // D,),
        in_specs=[pl.BlockSpec((B, D), lambda j: (0, 0)),
                  pl.BlockSpec((D, D), lambda j: (0, j)),
                  pl.BlockSpec((1, D), lambda j: (0, j))],
        out_specs=pl.BlockSpec((B, D), lambda j: (0, j)),
        out_shape=jax.ShapeDtypeStruct((B, n), F32),
        compiler_params=_cparams(("arbitrary",)),
        name="ada",
    )(c, w_ada, b_ada.reshape(1, n))


def _modulated_norm(x, g, shift, scale):
    ms = jnp.mean(x * x, axis=-1, keepdims=True)
    return (x * lax.rsqrt(ms + RMS_EPS) * g) * (1.0 + scale) + shift


def _inproj_kernel(x_ref, sh_ref, sc_ref, g_ref, w_ref, gq_ref, gk_ref, cos_ref, sin_ref, grp_ref,
                   u_ref, qt_ref, k_ref, vt_ref, km_ref):
    ts = x_ref.shape[1]
    h = _modulated_norm(x_ref[0], g_ref[...], sh_ref[0], sc_ref[0])
    hb = h.astype(BF16)

    def proj(lo):
        return jnp.dot(hb, w_ref[:, lo:lo + 512], preferred_element_type=F32)

    u_ref[0] = proj(0) * _sigmoid(proj(D_CONV))

    cos = cos_ref[...]
    sin = sin_ref[...]
    lane = lax.broadcasted_iota(I32, (1, D_ATTN), 1)
    first_half = (lane % HEAD_DIM) < (HEAD_DIM // 2)
    grp = grp_ref[...]

    def norm_rope(t, g):
        hi, lo = _split_bf16(t * t)
        ss = (jnp.dot(hi, grp, preferred_element_type=F32)
              + jnp.dot(lo, grp, preferred_element_type=F32))
        tn = t * lax.rsqrt(ss * (1.0 / HEAD_DIM) + RMS_EPS) * g
        partner = jnp.where(first_half,
                            pltpu.roll(tn, D_ATTN - HEAD_DIM // 2, 1),
                            pltpu.roll(tn, HEAD_DIM // 2, 1))
        return tn * cos + partner * sin

    q = norm_rope(proj(2 * D_CONV), gq_ref[...])
    k = norm_rope(proj(2 * D_CONV + D_ATTN), gk_ref[...])
    k_ref[0] = k.astype(BF16)
    qt = (q * (1.0 / math.sqrt(HEAD_DIM))).T.astype(BF16)
    vt = proj(2 * D_CONV + 2 * D_ATTN).T.astype(BF16)
    w = 2 * HEAD_DIM
    for j in range(ts // MOBA_BLOCK):
        km_ref[0, j] = jnp.mean(k[j * MOBA_BLOCK:(j + 1) * MOBA_BLOCK], axis=0, keepdims=True)
        for p in range(N_HEADS // 2):
            qt_ref[0, p, j] = qt[p * w:(p + 1) * w, j * MOBA_BLOCK:(j + 1) * MOBA_BLOCK]
            vt_ref[0, p, j] = vt[p * w:(p + 1) * w, j * MOBA_BLOCK:(j + 1) * MOBA_BLOCK]


def _in_proj(x, sh1, sc1, norm1_g, w_in_bf, gq, gk, cos_t, sin_t, grp_ones, ts):
    B, S, D = x.shape
    nS = S // ts
    nbt = ts // MOBA_BLOCK
    npair = N_HEADS // 2
    tile_t = (1, npair, nbt, 2 * HEAD_DIM, MOBA_BLOCK)
    shape_t = (B, npair, S // MOBA_BLOCK, 2 * HEAD_DIM, MOBA_BLOCK)
    row = lambda i, b: (b, i, 0)
    const = lambda i, b: (0, 0)
    return pl.pallas_call(
        _inproj_kernel,
        grid=(nS, B),
        in_specs=[pl.BlockSpec((1, ts, D), row),
                  pl.BlockSpec((1, 1, D), lambda i, b: (b, 0, 0)),
                  pl.BlockSpec((1, 1, D), lambda i, b: (b, 0, 0)),
                  pl.BlockSpec((1, D), const),
                  pl.BlockSpec(w_in_bf.shape, const),
                  pl.BlockSpec((1, D_ATTN), const),
                  pl.BlockSpec((1, D_ATTN), const),
                  pl.BlockSpec((ts, D_ATTN), lambda i, b: (i, 0)),
                  pl.BlockSpec((ts, D_ATTN), lambda i, b: (i, 0)),
                  pl.BlockSpec((D_ATTN, D_ATTN), const)],
        out_specs=[pl.BlockSpec((1, ts, D_CONV), row),
                   pl.BlockSpec(tile_t, lambda i, b: (b, 0, i, 0, 0)),
                   pl.BlockSpec((1, ts, D_ATTN), row),
                   pl.BlockSpec(tile_t, lambda i, b: (b, 0, i, 0, 0)),
                   pl.BlockSpec((1, nbt, 1, D_ATTN), lambda i, b: (b, i, 0, 0))],
        out_shape=[jax.ShapeDtypeStruct((B, S, D_CONV), F32),
                   jax.ShapeDtypeStruct(shape_t, BF16),
                   jax.ShapeDtypeStruct((B, S, D_ATTN), BF16),
                   jax.ShapeDtypeStruct(shape_t, BF16),
                   jax.ShapeDtypeStruct((B, S // MOBA_BLOCK, 1, D_ATTN), F32)],
        compiler_params=_cparams(("arbitrary", "arbitrary")),
        name="in_proj",
    )(x, sh1, sc1, norm1_g, w_in_bf, gq, gk, cos_t, sin_t, grp_ones)


def _conv_kernel(u_ref, halo_ref, w_ref, b_ref, lng_ref, lnb_ref, gg_ref, o_ref, buf_ref):
    ts = u_ref.shape[1]
    i = pl.program_id(1)
    halo = halo_ref[0]
    buf_ref[0:CONV_HALO] = jnp.where(i == 0, jnp.zeros_like(halo), halo)
    buf_ref[CONV_HALO:CONV_HALO + ts] = u_ref[0]
    base = CONV_HALO - (CONV_WIDTH - 1)
    acc = jnp.broadcast_to(b_ref[...], (ts, D_CONV))
    for j in range(CONV_WIDTH):
        acc = acc + w_ref[j:j + 1, :] * buf_ref[base + j:base + j + ts, :]
    mu = jnp.mean(acc, axis=-1, keepdims=True)
    xc = acc - mu
    var = jnp.mean(xc * xc, axis=-1, keepdims=True)
    y = xc * lax.rsqrt(var + LN_EPS) * lng_ref[...] + lnb_ref[...]
    s = y * _sigmoid(y)
    ms = jnp.mean(s * s, axis=-1, keepdims=True)
    o_ref[0] = (s * lax.rsqrt(ms + RMS_EPS) * gg_ref[...]).astype(o_ref.dtype)


def _conv_group(u, conv_w, conv_b, ln_g, ln_b, gg, ts):
    B, S, C = u.shape
    nS = S // ts
    hpt = ts // CONV_HALO
    const = lambda b, i: (0, 0)
    return pl.pallas_call(
        _conv_kernel,
        grid=(B, nS),
        in_specs=[pl.BlockSpec((1, ts, C), lambda b, i: (b, i, 0)),
                  pl.BlockSpec((1, CONV_HALO, C), lambda b, i: (b, jnp.maximum(i * hpt - 1, 0), 0)),
                  pl.BlockSpec((CONV_WIDTH, C), const),
                  pl.BlockSpec((1, C), const),
                  pl.BlockSpec((1, C), const),
                  pl.BlockSpec((1, C), const),
                  pl.BlockSpec((1, C), const)],
        out_specs=pl.BlockSpec((1, ts, C), lambda b, i: (b, i, 0)),
        out_shape=jax.ShapeDtypeStruct((B, S, C), BF16),
        scratch_shapes=[pltpu.VMEM((ts + CONV_HALO, C), F32)],
        compiler_params=_cparams(("arbitrary", "arbitrary")),
        name="conv",
    )(u, u, conv_w, conv_b, ln_g, ln_b, gg)


def _attn_kernel(qt_ref, k_ref, vt_ref, km_ref, o_ref, bias_ref):
    nb = km_ref.shape[1]
    qi = pl.program_id(2)
    blk = MOBA_BLOCK
    qt = qt_ref[0, 0, 0]
    drow = lax.broadcasted_iota(I32, (2 * HEAD_DIM, 1), 0)
    qh = [jnp.where((drow // HEAD_DIM) == h, qt, jnp.zeros_like(qt)) for h in range(2)]

    km = km_ref[0]
    k1 = km.astype(BF16)
    r1 = km - k1.astype(F32)
    k2 = r1.astype(BF16)
    k3 = (r1 - k2.astype(F32)).astype(BF16)
    bid = lax.broadcasted_iota(I32, (nb, blk), 0)
    past = bid < qi
    for h in range(2):
        gate = (jnp.dot(k1, qh[h], preferred_element_type=F32)
                + jnp.dot(k2, qh[h], preferred_element_type=F32)
                + jnp.dot(k3, qh[h], preferred_element_type=F32))
        gate = jnp.where(past, gate, -jnp.inf)
        rank = jnp.zeros((nb, blk), I32)
        for j in range(nb):
            gj = gate[j:j + 1, :]
            rank = rank + ((gj > gate) | ((gj == gate) & (j < bid))).astype(I32)
        bias_ref[h] = jnp.where((rank < MOBA_TOPK) & past, 0.0, NEG_INF)

    def pv(h, vt_blk, p):
        return jnp.dot(vt_blk[h * HEAD_DIM:(h + 1) * HEAD_DIM, :], p.astype(BF16),
                       preferred_element_type=F32)

    start = pl.multiple_of(qi * blk, blk)
    k_own = k_ref[0, pl.ds(start, blk), :]
    vt_own = vt_ref[0, 0, qi]
    causal = (lax.broadcasted_iota(I32, (blk, blk), 0) <= lax.broadcasted_iota(I32, (blk, blk), 1))
    state_a, state_b = [], []
    own = [jnp.where(causal, jnp.dot(k_own, qh[h], preferred_element_type=F32), NEG_INF) for h in range(2)]
    own_m = [jnp.max(s, axis=0, keepdims=True) for s in own]
    own_p = [jnp.exp(s - m) for s, m in zip(own, own_m)]
    for h in range(2):
        state_a += [own_m[h], jnp.sum(own_p[h], axis=0, keepdims=True), pv(h, vt_own, own_p[h])]
        state_b += [jnp.full((1, blk), NEG_INF, F32), jnp.zeros((1, blk), F32),
                    jnp.zeros((HEAD_DIM, blk), F32)]

    def body(i, carry):
        ja = 2 * i
        jb = jnp.minimum(ja + 1, qi - 1)
        b_real = (ja + 1) < qi
        ka = k_ref[0, pl.ds(pl.multiple_of(ja * blk, blk), blk), :]
        kb = k_ref[0, pl.ds(pl.multiple_of(jb * blk, blk), blk), :]
        vts = [vt_ref[0, 0, ja], vt_ref[0, 0, jb]]
        scores = []
        for stream, kj in enumerate((ka, kb)):
            for h in range(2):
                if stream == 0:
                    bias_row = bias_ref[h, pl.ds(ja, 1), :]
                else:
                    bias_row = jnp.where(b_real, bias_ref[h, pl.ds(jb, 1), :], NEG_INF)
                scores.append(jnp.dot(kj, qh[h], preferred_element_type=F32) + bias_row)
        probs, stats = [], []
        for c, s in enumerate(scores):
            m, l, _ = carry[3 * c:3 * c + 3]
            m_new = jnp.maximum(m, jnp.max(s, axis=0, keepdims=True))
            alpha = jnp.exp(m - m_new)
            p = jnp.exp(s - m_new)
            probs.append(p.astype(BF16))
            stats.append((m_new, alpha * l + jnp.sum(p, axis=0, keepdims=True), alpha))
        out = []
        for c in range(4):
            m_new, l_new, alpha = stats[c]
            out += [m_new, l_new, alpha * carry[3 * c + 2] + pv(c % 2, vts[c // 2], probs[c])]
        return tuple(out)

    carry = lax.fori_loop(0, (qi + 1) // 2, body, tuple(state_a + state_b))
    outs = []
    for h in range(2):
        ma, la, acca = carry[3 * h:3 * h + 3]
        mb, lb, accb = carry[6 + 3 * h:9 + 3 * h]
        m = jnp.maximum(ma, mb)
        wa = jnp.exp(ma - m)
        wb = jnp.exp(mb - m)
        outs.append((wa * acca + wb * accb) / (wa * la + wb * lb))
    out_t = jnp.concatenate(outs, axis=0)
    o_ref[0] = out_t.T.astype(o_ref.dtype)


def _moba_attention(qt, k, vt, kmean):
    B, S, _ = k.shape
    nb = S // MOBA_BLOCK
    npair = N_HEADS // 2
    w = 2 * HEAD_DIM
    return pl.pallas_call(
        _attn_kernel,
        grid=(B, npair, nb),
        in_specs=[pl.BlockSpec((1, 1, 1, w, MOBA_BLOCK), lambda b, p, i: (b, p, i, 0, 0)),
                  pl.BlockSpec((1, S, w), lambda b, p, i: (b, 0, p)),
                  pl.BlockSpec((1, 1, nb, w, MOBA_BLOCK), lambda b, p, i: (b, p, 0, 0, 0)),
                  pl.BlockSpec((1, nb, w), lambda b, p, i: (b, 0, p))],
        out_specs=pl.BlockSpec((1, MOBA_BLOCK, w), lambda b, p, i: (b, i, p)),
        out_shape=jax.ShapeDtypeStruct((B, S, D_ATTN), BF16),
        scratch_shapes=[pltpu.VMEM((2, nb, MOBA_BLOCK), F32)],
        compiler_params=_cparams(("arbitrary", "arbitrary", "arbitrary")),
        name="moba_attn",
    )(qt, k, vt, kmean)


def _route(logits_t, bias_col):
    n = logits_t.shape[1]
    per = N_EXPERTS // N_GROUP
    scores = _sigmoid(logits_t)
    biased = scores + bias_col
    g3 = biased.reshape(N_GROUP, per, n)
    sub = lax.broadcasted_iota(I32, (N_GROUP, per, n), 1)
    m1 = jnp.max(g3, axis=1, keepdims=True)
    first = jnp.min(jnp.where(g3 == m1, sub, per), axis=1, keepdims=True)
    m2 = jnp.max(jnp.where(sub == first, -jnp.inf, g3), axis=1)
    gs = m1[:, 0, :] + m2
    gi = lax.broadcasted_iota(I32, (N_GROUP, n), 0)
    grank = jnp.zeros((N_GROUP, n), I32)
    for j in range(N_GROUP):
        row = gs[j:j + 1, :]
        grank = grank + ((row > gs) | ((row == gs) & (j < gi))).astype(I32)
    gmask = grank < TOPK_GROUP
    masked = jnp.where(gmask[:, None, :], g3, -jnp.inf).reshape(N_EXPERTS, n)
    ei = lax.broadcasted_iota(I32, (N_EXPERTS, n), 0)
    ids, ws = [], []
    onehot = jnp.zeros((N_EXPERTS, n), F32)
    for _ in range(TOP_K):
        m = jnp.max(masked, axis=0, keepdims=True)
        idx = jnp.min(jnp.where(masked == m, ei, N_EXPERTS), axis=0, keepdims=True)
        hit = ei == idx
        ws.append(jnp.sum(jnp.where(hit, scores, 0.0), axis=0, keepdims=True))
        ids.append(idx)
        masked = jnp.where(hit, -jnp.inf, masked)
        onehot = onehot + hit.astype(F32)
    w = jnp.concatenate(ws, axis=0)
    w = w / jnp.sum(w, axis=0, keepdims=True) * ROUTED_SCALE
    return jnp.concatenate(ids, axis=0), w, onehot


def _outproj_kernel(conv_ref, attn_ref, x_ref, g1_ref, gga_ref, wo_ref, n2_ref, sh_ref, sc_ref,
                    wr_ref, rb_ref, tri_ref,
                    x1_ref, h2_ref, eid_ref, wt_ref, rank_ref, cnt_ref, run_ref):
    first_step = (pl.program_id(0) == 0) & (pl.program_id(1) == 0)

    @pl.when(first_step)
    def _():
        run_ref[...] = jnp.zeros_like(run_ref)

    at = attn_ref[0].astype(F32)
    ms = jnp.mean(at * at, axis=-1, keepdims=True)
    atn = (at * lax.rsqrt(ms + RMS_EPS) * gga_ref[...]).astype(BF16)
    mix = (jnp.dot(conv_ref[0], wo_ref[0:D_CONV, :], preferred_element_type=F32)
           + jnp.dot(atn, wo_ref[D_CONV:D_CONV + D_ATTN, :], preferred_element_type=F32))
    x1 = x_ref[0] + g1_ref[0] * mix
    x1_ref[0] = x1
    h2 = _modulated_norm(x1, n2_ref[...], sh_ref[0], sc_ref[0])
    h2_ref[0] = h2

    nt = (((1,), (1,)), ((), ()))
    hh, hl = _split_bf16(h2)
    wh, wl = _split_bf16(wr_ref[...])
    logits_t = (lax.dot_general(wh, hh, nt, preferred_element_type=F32)
                + lax.dot_general(wh, hl, nt, preferred_element_type=F32)
                + lax.dot_general(wl, hh, nt, preferred_element_type=F32))
    ids, w, onehot = _route(logits_t, rb_ref[...])
    eid_ref[...] = ids
    wt_ref[...] = w

    prefix = jnp.dot(onehot.astype(BF16), tri_ref[...], preferred_element_type=F32)
    slot = prefix + run_ref[...]
    ei = lax.broadcasted_iota(I32, onehot.shape, 0)
    ranks = [jnp.sum(jnp.where(ei == ids[r:r + 1, :], slot, 0.0), axis=0, keepdims=True)
             for r in range(TOP_K)]
    rank_ref[...] = jnp.concatenate(ranks, axis=0).astype(I32)
    run_ref[...] = run_ref[...] + jnp.sum(onehot, axis=1, keepdims=True)
    cnt_ref[...] = jnp.broadcast_to(run_ref[...], cnt_ref.shape)


def _out_proj_route(conv_o, attn_o, x, g1, gg_attn, w_out_bf, norm2_g, sh2, sc2, w_router_t,
                    router_bias, tri, ts):
    B, S, D = x.shape
    nS = S // ts
    T = B * S
    row = lambda b, i: (b, i, 0)
    perb = lambda b, i: (b, 0, 0)
    const = lambda b, i: (0, 0)
    tok = lambda b, i: (0, b * nS + i)
    return pl.pallas_call(
        _outproj_kernel,
        grid=(B, nS),
        in_specs=[pl.BlockSpec((1, ts, D_CONV), row),
                  pl.BlockSpec((1, ts, D_ATTN), row),
                  pl.BlockSpec((1, ts, D), row),
                  pl.BlockSpec((1, 1, D), perb),
                  pl.BlockSpec((1, D_ATTN), const),
                  pl.BlockSpec((D, D), const),
                  pl.BlockSpec((1, D), const),
                  pl.BlockSpec((1, 1, D), perb),
                  pl.BlockSpec((1, 1, D), perb),
                  pl.BlockSpec((N_EXPERTS, D), const),
                  pl.BlockSpec((N_EXPERTS, 1), const),
                  pl.BlockSpec((ts, ts), const)],
        out_specs=[pl.BlockSpec((1, ts, D), row),
                   pl.BlockSpec((1, ts, D), row),
                   pl.BlockSpec((TOP_K, ts), tok),
                   pl.BlockSpec((TOP_K, ts), tok),
                   pl.BlockSpec((TOP_K, ts), tok),
                   pl.BlockSpec((N_EXPERTS, 128), const)],
        out_shape=[jax.ShapeDtypeStruct((B, S, D), F32),
                   jax.ShapeDtypeStruct((B, S, D), F32),
                   jax.ShapeDtypeStruct((TOP_K, T), I32),
                   jax.ShapeDtypeStruct((TOP_K, T), F32),
                   jax.ShapeDtypeStruct((TOP_K, T), I32),
                   jax.ShapeDtypeStruct((N_EXPERTS, 128), F32)],
        scratch_shapes=[pltpu.VMEM((N_EXPERTS, 1), F32)],
        compiler_params=_cparams(("arbitrary", "arbitrary")),
        name="out_proj_route",
    )(conv_o, attn_o, x, g1, gg_attn, w_out_bf, norm2_g, sh2, sc2, w_router_t, router_bias, tri)


def _slots_kernel(eid_ref, rank_ref, poffs_ref, dest_ref):
    n = eid_ref.shape[1]
    ei = lax.broadcasted_iota(I32, (N_EXPERTS, n), 0)
    po = poffs_ref[...]
    offs = [jnp.sum(jnp.where(ei == eid_ref[k:k + 1, :], po, 0.0), axis=0, keepdims=True)
            for k in range(TOP_K)]
    dest_ref[...] = jnp.concatenate(offs, axis=0).astype(I32) + rank_ref[...]


def _slots(eid_t, rank_t, poffs, tn):
    T = eid_t.shape[1]
    tok = lambda i: (0, i)
    return pl.pallas_call(
        _slots_kernel,
        grid=(T // tn,),
        in_specs=[pl.BlockSpec((TOP_K, tn), tok),
                  pl.BlockSpec((TOP_K, tn), tok),
                  pl.BlockSpec((N_EXPERTS, 1), lambda i: (0, 0))],
        out_specs=pl.BlockSpec((TOP_K, tn), tok),
        out_shape=jax.ShapeDtypeStruct((TOP_K, T), I32),
        compiler_params=_cparams(("arbitrary",)),
        name="moe_slots",
    )(eid_t, rank_t, poffs.astype(F32).reshape(N_EXPERTS, 1))


def _dispatch_kernel(pcnt_ref, pcum_ref, dest_ref, h_ref, xs_ref, zero_ref, sem):
    tq = h_ref.shape[0]

    @pl.when(pl.program_id(0) == 0)
    def _():
        zero_ref[...] = jnp.zeros_like(zero_ref)

        def tail_copy(e):
            start = pl.multiple_of(pcum_ref[e] - EXPERT_ROWS, EXPERT_ROWS)
            return pltpu.make_async_copy(zero_ref, xs_ref.at[pl.ds(start, EXPERT_ROWS)], sem)

        def issue_tail(e, _):
            @pl.when(pcnt_ref[e] > 0)
            def _():
                tail_copy(e).start()
            return 0

        def drain_tail(e, _):
            @pl.when(pcnt_ref[e] > 0)
            def _():
                tail_copy(e).wait()
            return 0

        lax.fori_loop(0, N_EXPERTS, issue_tail, 0)
        lax.fori_loop(0, N_EXPERTS, drain_tail, 0)

    def row_copy(t, k):
        return pltpu.make_async_copy(h_ref.at[pl.ds(t, 1)], xs_ref.at[pl.ds(dest_ref[k, t], 1)], sem)

    def issue(t, _):
        for k in range(TOP_K):
            row_copy(t, k).start()
        return 0

    lax.fori_loop(0, tq, issue, 0)

    def drain(t, _):
        for k in range(TOP_K):
            row_copy(t, k).wait()
        return 0

    lax.fori_loop(0, tq, drain, 0)


def _dispatch(pcounts, pcum, dest, h2, n_rows, tq):
    T, D = h2.shape
    grid_spec = pltpu.PrefetchScalarGridSpec(
        num_scalar_prefetch=2,
        grid=(T // tq,),
        in_specs=[pl.BlockSpec((TOP_K, tq), lambda i, pc, pm: (0, i), memory_space=pltpu.SMEM),
                  pl.BlockSpec((tq, D), lambda i, pc, pm: (i, 0))],
        out_specs=pl.BlockSpec(memory_space=pl.ANY),
        scratch_shapes=[pltpu.VMEM((EXPERT_ROWS, D), F32), pltpu.SemaphoreType.DMA(())],
    )
    return pl.pallas_call(
        _dispatch_kernel,
        grid_spec=grid_spec,
        out_shape=jax.ShapeDtypeStruct((n_rows, D), F32),
        compiler_params=_cparams(("arbitrary",)),
        name="moe_dispatch",
    )(pcounts, pcum, dest, h2)


def _experts_kernel(be_ref, nused_ref, xs_ref, wg_ref, wu_ref, wd_ref, ys_ref, wg_s, wu_s, wd_s):
    i = pl.program_id(0)
    changed = (i == 0) | (be_ref[i] != be_ref[jnp.maximum(i - 1, 0)])

    @pl.when(changed)
    def _():
        wg_s[...] = wg_ref[0].astype(BF16)
        wu_s[...] = wu_ref[0].astype(BF16)
        wd_s[...] = wd_ref[0].astype(BF16)

    @pl.when(i < nused_ref[0])
    def _():
        x = xs_ref[...].astype(BF16)
        g = jnp.dot(x, wg_s[...], preferred_element_type=F32)
        u = jnp.dot(x, wu_s[...], preferred_element_type=F32)
        a = (g * _sigmoid(g) * u).astype(BF16)
        ys_ref[...] = jnp.dot(a, wd_s[...], preferred_element_type=F32)

    @pl.when(i >= nused_ref[0])
    def _():
        ys_ref[...] = jnp.zeros_like(ys_ref)


def _experts(blk_e, nused, xs, w_gate_e, w_up_e, w_down_e):
    P, D = xs.shape
    F = w_gate_e.shape[2]
    nblk = P // EXPERT_ROWS
    grid_spec = pltpu.PrefetchScalarGridSpec(
        num_scalar_prefetch=2,
        grid=(nblk,),
        in_specs=[pl.BlockSpec((EXPERT_ROWS, D), lambda i, be, nu: (jnp.minimum(i, nu[0] - 1), 0)),
                  pl.BlockSpec((1, D, F), lambda i, be, nu: (be[i], 0, 0)),
                  pl.BlockSpec((1, D, F), lambda i, be, nu: (be[i], 0, 0)),
                  pl.BlockSpec((1, F, D), lambda i, be, nu: (be[i], 0, 0))],
        out_specs=pl.BlockSpec((EXPERT_ROWS, D), lambda i, be, nu: (i, 0)),
        scratch_shapes=[pltpu.VMEM((D, F), BF16), pltpu.VMEM((D, F), BF16), pltpu.VMEM((F, D), BF16)],
    )
    return pl.pallas_call(
        _experts_kernel,
        grid_spec=grid_spec,
        out_shape=jax.ShapeDtypeStruct((P, D), F32),
        compiler_params=_cparams(("arbitrary",)),
        name="moe_experts",
    )(blk_e, nused, xs, w_gate_e, w_up_e, w_down_e)


def _combine_kernel(dest_ref, w_ref, x1_ref, h2_ref, g2_ref, wgs_ref, wus_ref, wds_ref, ys_ref,
                    o_ref, buf_ref, sem):
    tc = x1_ref.shape[0]

    def row_copy(t, k):
        return pltpu.make_async_copy(ys_ref.at[pl.ds(dest_ref[k, t], 1)],
                                     buf_ref.at[k, pl.ds(t, 1)], sem)

    def issue(t, _):
        for k in range(TOP_K):
            row_copy(t, k).start()
        return 0

    lax.fori_loop(0, tc, issue, 0)

    hb = h2_ref[...].astype(BF16)
    g = jnp.dot(hb, wgs_ref[...], preferred_element_type=F32)
    u = jnp.dot(hb, wus_ref[...], preferred_element_type=F32)
    shared = jnp.dot((g * _sigmoid(g) * u).astype(BF16), wds_ref[...], preferred_element_type=F32)

    def drain(t, _):
        for k in range(TOP_K):
            row_copy(t, k).wait()
        return 0

    lax.fori_loop(0, tc, drain, 0)

    w = w_ref[...]
    routed = w[:, 0:1] * buf_ref[0]
    for k in range(1, TOP_K):
        routed = routed + w[:, k:k + 1] * buf_ref[k]
    o_ref[...] = x1_ref[...] + g2_ref[0] * (shared + routed)


def _combine(dest, w_tok, x1, h2, g2, wgs, wus, wds, ys, tc, S):
    T, D = x1.shape
    F = wgs.shape[1]
    const = lambda i: (0, 0)
    return pl.pallas_call(
        _combine_kernel,
        grid=(T // tc,),
        in_specs=[pl.BlockSpec((TOP_K, tc), lambda i: (0, i), memory_space=pltpu.SMEM),
                  pl.BlockSpec((tc, TOP_K), lambda i: (i, 0)),
                  pl.BlockSpec((tc, D), lambda i: (i, 0)),
                  pl.BlockSpec((tc, D), lambda i: (i, 0)),
                  pl.BlockSpec((1, 1, D), lambda i: ((i * tc) // S, 0, 0)),
                  pl.BlockSpec((D, F), const),
                  pl.BlockSpec((D, F), const),
                  pl.BlockSpec((F, D), const),
                  pl.BlockSpec(memory_space=pl.ANY)],
        out_specs=pl.BlockSpec((tc, D), lambda i: (i, 0)),
        out_shape=jax.ShapeDtypeStruct((T, D), F32),
        scratch_shapes=[pltpu.VMEM((TOP_K, tc, D), F32), pltpu.SemaphoreType.DMA(())],
        compiler_params=_cparams(("arbitrary",)),
        name="moe_combine",
    )(dest, w_tok, x1, h2, g2, wgs, wus, wds, ys)


def _rope_tables(S):
    inv_freq = ROPE_THETA ** (-jnp.arange(0, HEAD_DIM, 2, dtype=F32) / HEAD_DIM)
    ang = jnp.arange(S, dtype=F32)[:, None] * inv_freq[None, :]
    cos, sin = jnp.cos(ang), jnp.sin(ang)
    cos_t = jnp.tile(jnp.concatenate([cos, cos], axis=1), (1, N_HEADS))
    sin_t = jnp.tile(jnp.concatenate([-sin, sin], axis=1), (1, N_HEADS))
    return cos_t, sin_t


def _layer(x, mod, norm1_g, w_in, conv_w, conv_b, conv_ln_g, conv_ln_b, q_norm_g, k_norm_g,
           grp_norm_g, w_out, norm2_g, w_router, router_bias, w_gate_e, w_up_e, w_down_e,
           w_gate_s, w_up_s, w_down_s):
    B, S, D = x.shape
    T = B * S
    ts = min(512, S)
    sh1, sc1, g1, sh2, sc2, g2 = [m.reshape(B, 1, D) for m in jnp.split(mod, 6, axis=-1)]
    cos_t, sin_t = _rope_tables(S)
    head_of = jnp.arange(D_ATTN) // HEAD_DIM
    grp_ones = (head_of[:, None] == head_of[None, :]).astype(BF16)
    row = lambda a: a.reshape(1, -1)

    u, qt, k, vt, kmean = _in_proj(x, sh1, sc1, row(norm1_g), w_in.astype(BF16),
                                 row(jnp.tile(q_norm_g, N_HEADS)), row(jnp.tile(k_norm_g, N_HEADS)),
                                 cos_t, sin_t, grp_ones, ts)
    conv_o = _conv_group(u, conv_w, row(conv_b), row(conv_ln_g), row(conv_ln_b),
                         row(grp_norm_g[:D_CONV]), ts)
    attn_o = _moba_attention(qt, k, vt, kmean.reshape(B, S // MOBA_BLOCK, D_ATTN))

    tri = (jnp.arange(ts)[:, None] < jnp.arange(ts)[None, :]).astype(BF16)
    x1, h2, eid_t, w_t, rank_t, cnt = _out_proj_route(
        conv_o, attn_o, x, g1, row(grp_norm_g[D_CONV:]), w_out.astype(BF16), row(norm2_g), sh2, sc2,
        w_router.T, router_bias.reshape(N_EXPERTS, 1), tri, ts)

    counts = cnt[:, 0].astype(I32)
    pcounts = (counts + EXPERT_ROWS - 1) // EXPERT_ROWS * EXPERT_ROWS
    pcum = jnp.cumsum(pcounts)
    poffs = pcum - pcounts
    dest = _slots(eid_t, rank_t, poffs, min(2048, T))
    n_rows = T * TOP_K + N_EXPERTS * EXPERT_ROWS
    nblk = n_rows // EXPERT_ROWS
    blk_start = jnp.arange(nblk, dtype=I32) * EXPERT_ROWS
    blk_e = jnp.minimum(jnp.sum((pcum[None, :] <= blk_start[:, None]).astype(I32), axis=1), N_EXPERTS - 1)
    nused = (pcum[-1:] // EXPERT_ROWS).astype(I32)

    h2f = h2.reshape(T, D)
    xs = _dispatch(pcounts, pcum, dest, h2f, n_rows, min(512, T))
    ys = _experts(blk_e, nused, xs, w_gate_e, w_up_e, w_down_e)
    out = _combine(dest, w_t.T, x1.reshape(T, D), h2f, g2,
                   w_gate_s.astype(BF16), w_up_s.astype(BF16), w_down_s.astype(BF16), ys,
                   min(128, T), S)
    return out.reshape(B, S, D)


def kernel(x, c, w_ada, b_ada, norm1_g, w_in, conv_w, conv_b, conv_ln_g, conv_ln_b, q_norm_g, k_norm_g,
           grp_norm_g, w_out, norm2_g, w_router, router_bias, w_gate_e, w_up_e, w_down_e,
           w_gate_s, w_up_s, w_down_s):
    for l in range(w_ada.shape[0]):
        mod = _ada(c, w_ada[l], b_ada[l])
        x = _layer(x, mod, norm1_g[l], w_in[l], conv_w[l], conv_b[l], conv_ln_g[l], conv_ln_b[l],
                   q_norm_g[l], k_norm_g[l], grp_norm_g[l], w_out[l], norm2_g[l], w_router[l],
                   router_bias[l], w_gate_e[l], w_up_e[l], w_down_e[l], w_gate_s[l], w_up_s[l],
                   w_down_s[l])
    return x
```
